```python
import math
import jax, jax.numpy as jnp
from jax import lax
import numpy as np

D_MODEL = 1024
BATCH = 8
SEQ = 2048
DEPTH = 4

CTX_LEN = 256
GRID_W = 64
HEAD_DIM = 64
BRANCH_WIDTH = D_MODEL // 2
N_BRANCH = 3
SSM_WIDTH = BRANCH_WIDTH
SSM_GROUP = 16
SSM_GROUPS = SSM_WIDTH // SSM_GROUP
SSM_STATE = 64
DT_MIN = 1e-3
DT_MAX = 1e-1
WIN_HEADS = BRANCH_WIDTH // HEAD_DIM
WIN_KV_HEADS = 2
WINDOW = 128
WIN_BLOCK = 128
NA_HEADS = BRANCH_WIDTH // HEAD_DIM
NA_KH = 8
NA_KW = 16
WIN_Q = WIN_HEADS * HEAD_DIM
WIN_KV = WIN_KV_HEADS * HEAD_DIM
NA_W = NA_HEADS * HEAD_DIM
IN_SPLITS = (SSM_WIDTH, WIN_Q, WIN_KV, WIN_KV, NA_W, NA_W, NA_W, N_BRANCH * D_MODEL)
IN_WIDTH = sum(IN_SPLITS)
N_EXPERTS = 32
TOP_K = 4
D_FF = D_MODEL
SWIGLU_ALPHA = 1.702
SWIGLU_LIMIT = 7.0
MOE_BLOCK = 128
ROPE_THETA = 10000.0
EPS = 1e-6
NEG_INF = -1e30

kernel_name = 'hybrid_ssm_window_natten_moe_dit'


def rms_norm(t, g):
    tf = t.astype(jnp.float32)
    y = tf * lax.rsqrt(jnp.mean(tf * tf, axis=-1, keepdims=True) + EPS) * g.astype(jnp.float32)
    return y.astype(t.dtype)


def rope_2d(t, row_pos, col_pos):
    half = t.shape[-1] // 2
    quarter = half // 2
    inv_freq = ROPE_THETA ** (-jnp.arange(quarter, dtype=jnp.float32) / quarter)

    def rotate(th, pos):
        ang = pos.astype(jnp.float32)[:, None] * inv_freq[None, :]
        cos = jnp.cos(ang)[None, :, None, :]
        sin = jnp.sin(ang)[None, :, None, :]
        t1 = th[..., :quarter].astype(jnp.float32)
        t2 = th[..., quarter:].astype(jnp.float32)
        return jnp.concatenate([t1 * cos - t2 * sin, t1 * sin + t2 * cos], axis=-1)

    out = jnp.concatenate([rotate(t[..., :half], row_pos), rotate(t[..., half:], col_pos)], axis=-1)
    return out.astype(t.dtype)


def complex_linear_combine(e1, e2):
    a1r, a1i, b1r, b1i = e1
    a2r, a2i, b2r, b2i = e2
    return (a2r * a1r - a2i * a1i,
            a2r * a1i + a2i * a1r,
            a2r * b1r - a2i * b1i + b2r,
            a2r * b1i + a2i * b1r + b2i)


def s5_mixer(u_l, u_c, need_ctx, lam_re, lam_im, log_dt, b_re, b_im, c_re, c_im, d_skip, w_glu):
    f32 = jnp.float32
    Bsz, L, P = u_l.shape
    C = u_c.shape[1]
    T = C + L
    lr = lam_re.astype(f32)
    li = lam_im.astype(f32)
    dt = jnp.exp(log_dt.astype(f32))[..., None]
    mag = jnp.exp(lr * dt)
    ar = mag * jnp.cos(li * dt)
    ai = mag * jnp.sin(li * dt)
    den = lr * lr + li * li
    fr = ((ar - 1.0) * lr + ai * li) / den
    fi = (ai * lr - (ar - 1.0) * li) / den
    br = b_re.astype(f32)
    bi = b_im.astype(f32)
    bbr = fr[..., None] * br - fi[..., None] * bi
    bbi = fr[..., None] * bi + fi[..., None] * br

    def scan_direction(dirn, seq):
        ug = seq.astype(f32).reshape(Bsz, T, SSM_GROUPS, SSM_GROUP)
        xr = jnp.einsum('btgp,gnp->tbgn', ug, bbr[dirn])
        xi = jnp.einsum('btgp,gnp->tbgn', ug, bbi[dirn])
        a_r = jnp.broadcast_to(ar[dirn][None, None], (T, 1, SSM_GROUPS, SSM_STATE))
        a_i = jnp.broadcast_to(ai[dirn][None, None], (T, 1, SSM_GROUPS, SSM_STATE))
        _, _, hr, hi = lax.associative_scan(complex_linear_combine, (a_r, a_i, xr, xi), axis=0)
        y = (jnp.einsum('tbgn,gpn->btgp', hr, c_re[dirn].astype(f32))
             - jnp.einsum('tbgn,gpn->btgp', hi, c_im[dirn].astype(f32)))
        return y.reshape(Bsz, T, P)

    y_f = scan_direction(0, jnp.concatenate([u_c, u_l], axis=1))
    y_b = scan_direction(1, jnp.concatenate([jnp.flip(u_c, 1), jnp.flip(u_l, 1)], axis=1))
    y_b_c = jnp.flip(y_b[:, :C], 1)
    y_b_l = jnp.flip(y_b[:, C:], 1)

    def readout(yf, yb, u):
        y = yf + yb + d_skip.astype(f32) * u.astype(f32)
        g = jax.nn.gelu(y).astype(u.dtype)
        return g * jax.nn.sigmoid(g @ w_glu)

    y_l = readout(y_f[:, C:], y_b_l, u_l)
    y_c = readout(y_f[:, :C], y_b_c, u_c) if need_ctx else None
    return y_l, y_c


def context_attention(q, k, v, sink):
    Bsz, C, H, d = q.shape
    hkv = k.shape[2]
    grp = H // hkv
    qg = q.reshape(Bsz, C, hkv, grp, d)
    s = jnp.einsum('bqhgd,bkhd->bhgqk', qg, k, preferred_element_type=jnp.float32) * (d ** -0.5)
    if sink is not None:
        s_sink = jnp.broadcast_to(sink.astype(jnp.float32).reshape(1, hkv, grp, 1, 1), s.shape[:-1] + (1,))
        s = jnp.concatenate([s, s_sink], axis=-1)
    p = jax.nn.softmax(s, axis=-1)[..., :C].astype(v.dtype)
    o = jnp.einsum('bhgqk,bkhd->bqhgd', p, v)
    return o.reshape(Bsz, C, H * d)


def window_attention(q, k, v, k_ctx, v_ctx, sink):
    Bsz, L, H, d = q.shape
    hkv = k.shape[2]
    grp = H // hkv
    nb = L // WIN_BLOCK
    C = k_ctx.shape[1]
    nk = 3 * WIN_BLOCK
    qb = q.reshape(Bsz, nb, WIN_BLOCK, hkv, grp, d)

    def band(t):
        tp = jnp.pad(t, ((0, 0), (WIN_BLOCK, WIN_BLOCK), (0, 0), (0, 0))).reshape(Bsz, nb + 2, WIN_BLOCK, hkv, d)
        return jnp.concatenate([tp[:, :-2], tp[:, 1:-1], tp[:, 2:]], axis=2)

    kb, vb = band(k), band(v)
    qpos = jnp.arange(L).reshape(nb, WIN_BLOCK)
    kpos = jnp.arange(nb)[:, None] * WIN_BLOCK - WIN_BLOCK + jnp.arange(nk)[None, :]
    valid = ((kpos[:, None, :] >= 0) & (kpos[:, None, :] < L)
             & (jnp.abs(qpos[:, :, None] - kpos[:, None, :]) <= WINDOW))
    scale = d ** -0.5
    s_loc = jnp.einsum('bnqhgd,bnkhd->bnhgqk', qb, kb, preferred_element_type=jnp.float32) * scale
    s_loc = jnp.where(valid[None, :, None, None], s_loc, NEG_INF)
    s_ctx = jnp.einsum('bnqhgd,bchd->bnhgqc', qb, k_ctx, preferred_element_type=jnp.float32) * scale
    s_sink = jnp.broadcast_to(sink.astype(jnp.float32).reshape(1, 1, hkv, grp, 1, 1), s_loc.shape[:-1] + (1,))
    p = jax.nn.softmax(jnp.concatenate([s_loc, s_ctx, s_sink], axis=-1), axis=-1).astype(v.dtype)
    o = (jnp.einsum('bnhgqk,bnkhd->bnqhgd', p[..., :nk], vb)
         + jnp.einsum('bnhgqc,bchd->bnqhgd', p[..., nk:nk + C], v_ctx))
    return o.reshape(Bsz, L, H * d)


def neighbourhood_attention(q, k, v, k_ctx, v_ctx, rpb):
    Bsz, L, H, d = q.shape
    rows = L // GRID_W
    kh = min(NA_KH, rows)
    C = k_ctx.shape[1]
    scale = d ** -0.5

    def grid(t):
        return t.reshape(Bsz, rows, GRID_W, H, d)

    qg, kg, vg = grid(q), grid(k), grid(v)
    r = jnp.arange(rows)
    row_idx = jnp.clip(r - kh // 2, 0, rows - kh)[:, None] + jnp.arange(kh)[None, :]
    k_rows = kg[:, row_idx]
    v_rows = vg[:, row_idx]
    col = jnp.arange(GRID_W)
    col_start = jnp.clip(col - NA_KW // 2, 0, GRID_W - NA_KW)
    col_in = (col[None, :] >= col_start[:, None]) & (col[None, :] < col_start[:, None] + NA_KW)
    d_row = row_idx - r[:, None] + (NA_KH - 1)
    d_col = jnp.clip(col[None, :] - col[:, None], 1 - NA_KW, NA_KW - 1) + (NA_KW - 1)
    bias = rpb.astype(jnp.float32)[:, d_row[:, None, :, None], d_col[None, :, None, :]]
    s = jnp.einsum('brchd,brkwhd->bhrckw', qg, k_rows, preferred_element_type=jnp.float32) * scale + bias[None]
    s = jnp.where(col_in[:, None, :], s, NEG_INF)
    nloc = kh * GRID_W
    s = s.reshape(Bsz, H, rows, GRID_W, nloc)
    s_ctx = jnp.einsum('brchd,bshd->bhrcs', qg, k_ctx, preferred_element_type=jnp.float32) * scale
    p = jax.nn.softmax(jnp.concatenate([s, s_ctx], axis=-1), axis=-1).astype(v.dtype)
    p_loc = p[..., :nloc].reshape(Bsz, H, rows, GRID_W, kh, GRID_W)
    o = (jnp.einsum('bhrckw,brkwhd->brchd', p_loc, v_rows)
         + jnp.einsum('bhrcs,bshd->brchd', p[..., nloc:nloc + C], v_ctx))
    return o.reshape(Bsz, L, H * d)


def token_mixing(h_l, h_c, row_pos, col_pos, need_ctx, w_in, lam_re, lam_im, log_dt, b_re, b_im,
                 c_re, c_im, d_skip, w_glu, win_qn, win_kn, win_sink, na_qn, na_kn, na_rpb, w_branch, w_out):
    split_pts = [int(s) for s in np.cumsum(IN_SPLITS)[:-1]]
    a_l, wq_l, wk_l, wv_l, nq_l, nk_l, nv_l, gate_l = jnp.split(h_l @ w_in, split_pts, axis=-1)
    a_c, wq_c, wk_c, wv_c, nq_c, nk_c, nv_c, gate_c = jnp.split(h_c @ w_in, split_pts, axis=-1)

    def heads(t, n):
        return t.reshape(t.shape[0], t.shape[1], n, HEAD_DIM)

    y_a_l, y_a_c = s5_mixer(a_l, a_c, need_ctx, lam_re, lam_im, log_dt, b_re, b_im, c_re, c_im, d_skip, w_glu)

    q_b = rope_2d(rms_norm(heads(wq_l, WIN_HEADS), win_qn), row_pos, col_pos)
    k_b = rope_2d(rms_norm(heads(wk_l, WIN_KV_HEADS), win_kn), row_pos, col_pos)
    v_b = heads(wv_l, WIN_KV_HEADS)
    k_bc = rms_norm(heads(wk_c, WIN_KV_HEADS), win_kn)
    v_bc = heads(wv_c, WIN_KV_HEADS)
    y_b_l = window_attention(q_b, k_b, v_b, k_bc, v_bc, win_sink)

    q_n = rms_norm(heads(nq_l, NA_HEADS), na_qn)
    k_n = rms_norm(heads(nk_l, NA_HEADS), na_kn)
    v_n = heads(nv_l, NA_HEADS)
    k_nc = rms_norm(heads(nk_c, NA_HEADS), na_kn)
    v_nc = heads(nv_c, NA_HEADS)
    y_n_l = neighbourhood_attention(q_n, k_n, v_n, k_nc, v_nc, na_rpb)

    def merge(ys, gates):
        stacked = jnp.stack(ys, axis=2)
        proj = jnp.einsum('btip,ipd->btid', stacked, w_branch)
        g = jax.nn.sigmoid(gates.reshape(gates.shape[0], gates.shape[1], N_BRANCH, D_MODEL))
        return jnp.sum(g * proj, axis=2) @ w_out

    m_l = merge([y_a_l, y_b_l, y_n_l], gate_l)
    if not need_ctx:
        return m_l, None
    y_b_c = context_attention(rms_norm(heads(wq_c, WIN_HEADS), win_qn), k_bc, v_bc, win_sink)
    y_n_c = context_attention(rms_norm(heads(nq_c, NA_HEADS), na_qn), k_nc, v_nc, None)
    m_c = merge([y_a_c, y_b_c, y_n_c], gate_c)
    return m_l, m_c


def moe_ffn(h, router_w, router_b, w_gate_up, b_gate_up, w_down, b_down):
    T, D = h.shape
    logits = (h @ router_w + router_b).astype(jnp.float32)
    top_val, top_idx = lax.top_k(logits, TOP_K)
    gates = jax.nn.softmax(top_val, axis=-1)
    TK = T * TOP_K
    flat_e = top_idx.reshape(TK)
    flat_tok = jnp.arange(TK, dtype=jnp.int32) // TOP_K
    order = jnp.argsort(flat_e)
    sorted_e = flat_e[order]
    sorted_tok = flat_tok[order]
    counts = jnp.bincount(flat_e, length=N_EXPERTS)
    padded = (counts + MOE_BLOCK - 1) // MOE_BLOCK * MOE_BLOCK
    seg_end = jnp.cumsum(padded)
    pad_start = seg_end - padded
    start = jnp.cumsum(counts) - counts
    dest = pad_start[sorted_e] + jnp.arange(TK, dtype=jnp.int32) - start[sorted_e]
    n_blocks = -(-TK // MOE_BLOCK) + N_EXPERTS
    buf_tok = jnp.full((n_blocks * MOE_BLOCK,), T, jnp.int32).at[dest].set(sorted_tok)
    block_expert = jnp.minimum(jnp.searchsorted(seg_end, jnp.arange(n_blocks) * MOE_BLOCK, side='right'),
                               N_EXPERTS - 1)
    h_pad = jnp.concatenate([h, jnp.zeros((1, D), h.dtype)], axis=0)
    xb = h_pad[buf_tok].reshape(n_blocks, MOE_BLOCK, D)

    def expert_block(args):
        xblk, e = args
        gu = xblk @ w_gate_up[e] + b_gate_up[e]
        g = jnp.minimum(gu[:, :D_FF], SWIGLU_LIMIT)
        u = jnp.clip(gu[:, D_FF:], -SWIGLU_LIMIT, SWIGLU_LIMIT)
        act = (u + 1.0) * (g * jax.nn.sigmoid(SWIGLU_ALPHA * g))
        return act @ w_down[e] + b_down[e]

    yb = lax.map(expert_block, (xb, block_expert)).reshape(n_blocks * MOE_BLOCK, D)
    y = yb[dest] * gates.reshape(TK)[order][:, None].astype(h.dtype)
    return jax.ops.segment_sum(y, sorted_tok, num_segments=T)


def setup_inputs(seed: int = 0) -> dict:
    key = jax.random.key(seed)
    ks = list(jax.random.split(key, 32))
    f32 = jnp.float32

    def nrm(i, shape, s):
        return jax.random.normal(ks[i], shape, f32) * s

    G, N, Pg = SSM_GROUPS, SSM_STATE, SSM_GROUP
    lam_im = jnp.broadcast_to(jnp.pi * jnp.arange(N, dtype=f32), (DEPTH, 2, G, N))
    return {
        'x': nrm(0, (BATCH, SEQ, D_MODEL), 1.0),
        'c': nrm(1, (BATCH, D_MODEL), 1.0),
        'ctx': nrm(2, (BATCH, CTX_LEN, D_MODEL), 1.0),
        'c_ctx': nrm(3, (D_MODEL,), 1.0),
        'w_mod': nrm(4, (DEPTH, D_MODEL, 6 * D_MODEL), 0.5 * D_MODEL ** -0.5),
        'b_mod': nrm(5, (DEPTH, 6 * D_MODEL), 0.02),
        'norm_mix': 1.0 + nrm(6, (DEPTH, D_MODEL), 0.05),
        'norm_ffn': 1.0 + nrm(7, (DEPTH, D_MODEL), 0.05),
        'w_in': nrm(8, (DEPTH, D_MODEL, IN_WIDTH), D_MODEL ** -0.5),
        'ssm_lam_re': -0.5 + nrm(9, (DEPTH, 2, G, N), 0.02),
        'ssm_lam_im': lam_im,
        'ssm_log_dt': jax.random.uniform(ks[10], (DEPTH, 2, G), f32, math.log(DT_MIN), math.log(DT_MAX)),
        'ssm_b_re': nrm(11, (DEPTH, 2, G, N, Pg), Pg ** -0.5),
        'ssm_b_im': nrm(12, (DEPTH, 2, G, N, Pg), Pg ** -0.5),
        'ssm_c_re': nrm(13, (DEPTH, 2, G, Pg, N), N ** -0.5),
        'ssm_c_im': nrm(14, (DEPTH, 2, G, Pg, N), N ** -0.5),
        'ssm_d': nrm(15, (DEPTH, SSM_WIDTH), 1.0),
        'ssm_w_glu': nrm(16, (DEPTH, SSM_WIDTH, SSM_WIDTH), SSM_WIDTH ** -0.5),
        'win_q_norm': 1.0 + nrm(17, (DEPTH, HEAD_DIM), 0.05),
        'win_k_norm': 1.0 + nrm(18, (DEPTH, HEAD_DIM), 0.05),
        'win_sink': nrm(19, (DEPTH, WIN_HEADS), 0.5),
        'na_q_norm': 1.0 + nrm(20, (DEPTH, HEAD_DIM), 0.05),
        'na_k_norm': 1.0 + nrm(21, (DEPTH, HEAD_DIM), 0.05),
        'na_rpb': nrm(22, (DEPTH, NA_HEADS, 2 * NA_KH - 1, 2 * NA_KW - 1), 0.5),
        'w_branch': nrm(23, (DEPTH, N_BRANCH, BRANCH_WIDTH, D_MODEL), BRANCH_WIDTH ** -0.5),
        'w_out': nrm(24, (DEPTH, D_MODEL, D_MODEL), D_MODEL ** -0.5),
        'router_w': nrm(25, (DEPTH, D_MODEL, N_EXPERTS), D_MODEL ** -0.5),
        'router_b': nrm(26, (DEPTH, N_EXPERTS), 0.01),
        'w_gate_up': nrm(27, (DEPTH, N_EXPERTS, D_MODEL, 2 * D_FF), D_MODEL ** -0.5),
        'b_gate_up': nrm(28, (DEPTH, N_EXPERTS, 2 * D_FF), 0.01),
        'w_down': nrm(29, (DEPTH, N_EXPERTS, D_FF, D_MODEL), D_FF ** -0.5),
        'b_down': nrm(30, (DEPTH, N_EXPERTS, D_MODEL), 0.01),
    }


def reference(x, c, ctx, c_ctx, w_mod, b_mod, norm_mix, norm_ffn, w_in, ssm_lam_re, ssm_lam_im, ssm_log_dt,
              ssm_b_re, ssm_b_im, ssm_c_re, ssm_c_im, ssm_d, ssm_w_glu, win_q_norm, win_k_norm, win_sink,
              na_q_norm, na_k_norm, na_rpb, w_branch, w_out, router_w, router_b, w_gate_up, b_gate_up,
              w_down, b_down):
    Bsz, L, D = x.shape
    C = ctx.shape[1]
    pos = jnp.arange(L)
    row_pos = pos // GRID_W
    col_pos = pos % GRID_W
    lat, cx = x, ctx
    for li in range(DEPTH):
        need_ctx = li < DEPTH - 1
        mod_l = jnp.split((jax.nn.silu(c) @ w_mod[li] + b_mod[li])[:, None, :], 6, axis=-1)
        mod_c = jnp.split(jax.nn.silu(c_ctx) @ w_mod[li] + b_mod[li], 6, axis=-1)
        h_l = rms_norm(lat, norm_mix[li]) * (1.0 + mod_l[1]) + mod_l[0]
        h_c = rms_norm(cx, norm_mix[li]) * (1.0 + mod_c[1]) + mod_c[0]
        m_l, m_c = token_mixing(h_l, h_c, row_pos, col_pos, need_ctx, w_in[li], ssm_lam_re[li], ssm_lam_im[li],
                                ssm_log_dt[li], ssm_b_re[li], ssm_b_im[li], ssm_c_re[li], ssm_c_im[li], ssm_d[li],
                                ssm_w_glu[li], win_q_norm[li], win_k_norm[li], win_sink[li], na_q_norm[li],
                                na_k_norm[li], na_rpb[li], w_branch[li], w_out[li])
        lat = lat + mod_l[2] * m_l
        f_l = rms_norm(lat, norm_ffn[li]) * (1.0 + mod_l[4]) + mod_l[3]
        if need_ctx:
            cx = cx + mod_c[2] * m_c
            f_c = rms_norm(cx, norm_ffn[li]) * (1.0 + mod_c[4]) + mod_c[3]
            toks = jnp.concatenate([f_l.reshape(Bsz * L, D), f_c.reshape(Bsz * C, D)], axis=0)
        else:
            toks = f_l.reshape(Bsz * L, D)
        y = moe_ffn(toks, router_w[li], router_b[li], w_gate_up[li], b_gate_up[li], w_down[li], b_down[li])
        lat = lat + mod_l[5] * y[:Bsz * L].reshape(Bsz, L, D)
        if need_ctx:
            cx = cx + mod_c[5] * y[Bsz * L:].reshape(Bsz, C, D)
    return lat
```

```python
import functools
import math

import numpy as np
import jax
import jax.numpy as jnp
from jax import lax
from jax.experimental import pallas as pl
from jax.experimental.pallas import tpu as pltpu

F32 = jnp.float32
BF16 = jnp.bfloat16

HEAD_DIM = 64
GRID_W = 64
SSM_GROUP = 16
SSM_STATE = 64
WIN_HEADS = 8
WIN_KV_HEADS = 2
WINDOW = 128
NA_HEADS = 8
NA_KH = 8
NA_KW = 16
N_BRANCH = 3
N_EXPERTS = 32
TOP_K = 4
SWIGLU_ALPHA = 1.702
SWIGLU_LIMIT = 7.0
ROPE_THETA = 10000.0
EPS = 1e-6
NEG_INF = -1e30

LANES = 128
ATT_BLK = 128
ROW_TILE = 256
SSM_CHUNK = 64
SSM_STRIP = 512
MOE_BLK = 256
DMA_ROWS = 1024
VMEM_LIMIT = 56 * 1024 * 1024


def _params(sem, vmem=VMEM_LIMIT):
    return pltpu.CompilerParams(dimension_semantics=sem, vmem_limit_bytes=vmem)


def _mod_kernel(c_ref, w_ref, b_ref, o_ref):
    c = c_ref[...]
    a = (c * jax.nn.sigmoid(c)).astype(BF16)
    o_ref[...] = jnp.dot(a, w_ref[...].astype(BF16), preferred_element_type=F32) + b_ref[...]


def _modulation(cc, w_mod, b_mod):
    depth, d, n = w_mod.shape
    rows = cc.shape[0]
    tn = 1536
    return pl.pallas_call(
        _mod_kernel,
        grid=(depth, n // tn),
        in_specs=[pl.BlockSpec((rows, d), lambda l, j: (0, 0)),
                  pl.BlockSpec((None, d, tn), lambda l, j: (l, 0, j)),
                  pl.BlockSpec((None, 1, tn), lambda l, j: (l, 0, j))],
        out_specs=pl.BlockSpec((None, rows, tn), lambda l, j: (l, 0, j)),
        out_shape=jax.ShapeDtypeStruct((depth, rows, n), F32),
        compiler_params=_params(("parallel", "parallel")),
        name="modulation",
    )(cc, w_mod, b_mod.reshape(depth, 1, n))


def _rms_mod(x, g, mod_ref):
    ms = jnp.mean(x * x, axis=-1, keepdims=True)
    return (x * lax.rsqrt(ms + EPS) * g) * (1.0 + mod_ref[1:2, :]) + mod_ref[0:1, :]


def _head_norm(t, bd, w):
    sq = t * t
    hi = sq.astype(BF16)
    lo = (sq - hi.astype(F32)).astype(BF16)
    ms = jnp.dot(hi, bd, preferred_element_type=F32) + jnp.dot(lo, bd, preferred_element_type=F32)
    return t * lax.rsqrt(ms + EPS) * w


def _rope(t, cos, sin_a, sin_b):
    up = pltpu.roll(t, LANES - 16, 1)
    dn = pltpu.roll(t, 16, 1)
    return t * cos + up * sin_a + dn * sin_b


def _premix_kernel(s_ref, mod_ref, g_ref, w_ref, bd_ref, cos_ref, sa_ref, sb_ref,
                   wqn_ref, wkn_ref, nqn_ref, nkn_ref,
                   a_ref, wq_ref, wk_ref, wv_ref, nq_ref, nk_ref, nv_ref, gt_ref):
    d = s_ref.shape[-1]
    half = d // 2
    h = _rms_mod(s_ref[...], g_ref[...], mod_ref).astype(BF16)

    def seg(lo, width):
        return jnp.dot(h, w_ref[:, lo:lo + width], preferred_element_type=F32)

    cos, sa, sb = cos_ref[...], sa_ref[...], sb_ref[...]
    bd = bd_ref[...]
    scale = HEAD_DIM ** -0.5

    a_ref[...] = seg(0, half).astype(BF16)

    q = _head_norm(seg(half, half), bd, wqn_ref[...])
    for j in range(half // LANES):
        cs = slice(j * LANES, (j + 1) * LANES)
        wq_ref[:, cs] = (_rope(q[:, cs], cos, sa, sb) * scale).astype(BF16)

    kv = seg(2 * half, 2 * LANES)
    lane = lax.broadcasted_iota(jnp.int32, (kv.shape[0], LANES), 1)
    low = lane < HEAD_DIM

    def dup(t, out_ref):
        sw = pltpu.roll(t, HEAD_DIM, 1)
        out_ref[:, :LANES] = jnp.where(low, t, sw).astype(BF16)
        out_ref[:, LANES:] = jnp.where(low, sw, t).astype(BF16)

    k = _head_norm(kv[:, :LANES], bd[:LANES, :LANES], wkn_ref[...])
    dup(_rope(k, cos, sa, sb), wk_ref)
    dup(kv[:, LANES:], wv_ref)

    base = 2 * half + 2 * LANES
    nq_ref[...] = (_head_norm(seg(base, half), bd, nqn_ref[...]) * scale).astype(BF16)
    nk_ref[...] = _head_norm(seg(base + half, half), bd, nkn_ref[...]).astype(BF16)
    nv_ref[...] = seg(base + 2 * half, half).astype(BF16)
    base += 3 * half
    for j in range(N_BRANCH):
        gt_ref[:, j * d:(j + 1) * d] = jax.nn.sigmoid(seg(base + j * d, d)).astype(BF16)


def _premix(s, mod_a, g, w_in, bd, rope, qk, n_ctx):
    bsz, t, d = s.shape
    half = d // 2
    tm = ROW_TILE
    nct = n_ctx // tm
    cos, sa, sb = rope
    row = lambda b, i: (b, i, 0)
    const = lambda b, i: (0, 0)
    tab = pl.BlockSpec((tm, LANES), lambda b, i: (i, 0))
    vec = lambda n: pl.BlockSpec((1, n), const)

    def out(width):
        return pl.BlockSpec((None, tm, width), row), jax.ShapeDtypeStruct((bsz, t, width), BF16)

    outs = [(pl.BlockSpec((tm, half), lambda b, i: (i, b)), jax.ShapeDtypeStruct((t, bsz * half), BF16)),
            out(half), out(2 * LANES), out(2 * LANES), out(half), out(half), out(half), out(N_BRANCH * d)]
    return pl.pallas_call(
        _premix_kernel,
        grid=(bsz, t // tm),
        in_specs=[pl.BlockSpec((None, tm, d), row),
                  pl.BlockSpec((None, None, 2, d), lambda b, i: (b, (i >= nct).astype(jnp.int32), 0, 0)),
                  vec(d),
                  pl.BlockSpec(w_in.shape, const),
                  pl.BlockSpec(bd.shape, const),
                  tab, tab, tab,
                  vec(half), vec(LANES), vec(half), vec(half)],
        out_specs=[o[0] for o in outs],
        out_shape=[o[1] for o in outs],
        compiler_params=_params(("parallel", "parallel")),
        name="premix",
    )(s, mod_a, g, w_in, bd, cos, sa, sb, *qk)


def _ssm_kernel(u_ref, bb_ref, cc_ref, a_ref, y_ref, x_scr, h_scr, *, tc, nb):
    dirn = pl.program_id(0)
    p = u_ref.shape[-1]
    ns = a_ref.shape[-1]
    hp, hs = p // 2, ns // 2

    @pl.when(pl.program_id(1) == 0)
    def _():
        h_scr[...] = jnp.zeros_like(h_scr)

    u = u_ref[...]
    for part in range(2):
        for hf in range(2):
            x_scr[part, :, hf * hs:(hf + 1) * hs] = jnp.dot(
                u[:, hf * hp:(hf + 1) * hp],
                bb_ref[hf * hp:(hf + 1) * hp, part * ns + hf * hs:part * ns + (hf + 1) * hs],
                preferred_element_type=F32)

    for s in range(ns // SSM_STRIP):
        cs = slice(s * SSM_STRIP, (s + 1) * SSM_STRIP)
        ar = jnp.broadcast_to(a_ref[0:1, cs], (nb, SSM_STRIP))
        ai = jnp.broadcast_to(a_ref[1:2, cs], (nb, SSM_STRIP))

        def step(j, carry):
            hr, hi = carry
            t = jnp.where(dirn == 0, j, tc - 1 - j)
            r0 = pl.multiple_of(t * nb, nb)
            nr = ar * hr - ai * hi + x_scr[0, pl.ds(r0, nb), cs]
            ni = ar * hi + ai * hr + x_scr[1, pl.ds(r0, nb), cs]
            x_scr[0, pl.ds(r0, nb), cs] = nr
            x_scr[1, pl.ds(r0, nb), cs] = ni
            return nr, ni

        hr, hi = lax.fori_loop(0, tc, step, (h_scr[0, :, cs], h_scr[1, :, cs]), unroll=4)
        h_scr[0, :, cs] = hr
        h_scr[1, :, cs] = hi

    for hf in range(2):
        ss = slice(hf * hs, (hf + 1) * hs)
        ps = slice(hf * hp, (hf + 1) * hp)
        y = jnp.dot(x_scr[0, :, ss].astype(BF16), cc_ref[hf * hs:(hf + 1) * hs, ps],
                    preferred_element_type=F32)
        y += jnp.dot(x_scr[1, :, ss].astype(BF16), cc_ref[ns + hf * hs:ns + (hf + 1) * hs, ps],
                     preferred_element_type=F32)
        y_ref[:, ps] = y.astype(BF16)


def _ssm(u_tm, bb, cc, a, nb, n_ctx):
    rows, p = u_tm.shape
    t = rows // nb
    tc = SSM_CHUNK
    nch, ncc = t // tc, n_ctx // tc
    ns = a.shape[-1]

    def chunk(d, i):
        back = jnp.where(i < ncc, ncc - 1 - i, nch - 1 - (i - ncc))
        return jnp.where(d == 0, i, back)

    return pl.pallas_call(
        functools.partial(_ssm_kernel, tc=tc, nb=nb),
        grid=(2, nch),
        in_specs=[pl.BlockSpec((tc * nb, p), lambda d, i: (chunk(d, i), 0)),
                  pl.BlockSpec((None, p, 2 * ns), lambda d, i: (d, 0, 0)),
                  pl.BlockSpec((None, 2 * ns, p), lambda d, i: (d, 0, 0)),
                  pl.BlockSpec((None, 2, ns), lambda d, i: (d, 0, 0))],
        out_specs=pl.BlockSpec((None, tc * nb, p), lambda d, i: (d, chunk(d, i), 0)),
        out_shape=jax.ShapeDtypeStruct((2, rows, p), BF16),
        scratch_shapes=[pltpu.VMEM((2, tc * nb, ns), F32), pltpu.VMEM((2, nb, ns), F32)],
        compiler_params=_params(("arbitrary", "arbitrary")),
        name="ssm",
    )(u_tm, bb, cc, a)


def _ssm_params(lam_re, lam_im, log_dt, b_re, b_im, c_re, c_im):
    dt = jnp.exp(log_dt.astype(F32))[..., None]
    lr, li = lam_re.astype(F32), lam_im.astype(F32)
    mag = jnp.exp(lr * dt)
    ar, ai = mag * jnp.cos(li * dt), mag * jnp.sin(li * dt)
    den = lr * lr + li * li
    fr = ((ar - 1.0) * lr + ai * li) / den
    fi = (ai * lr - (ar - 1.0) * li) / den
    br, bi = b_re.astype(F32), b_im.astype(F32)
    bbr = fr[..., None] * br - fi[..., None] * bi
    bbi = fr[..., None] * bi + fi[..., None] * br
    g, n, pg = bbr.shape[1:]
    eye = jnp.eye(g, dtype=F32)

    def bdiag_in(m):
        return jnp.einsum('dgnp,gh->dgphn', m, eye).reshape(2, g * pg, g * n)

    def bdiag_out(m):
        return jnp.einsum('dgpn,gh->dgnhp', m, eye).reshape(2, g * n, g * pg)

    bb = jnp.concatenate([bdiag_in(bbr), bdiag_in(bbi)], axis=2).astype(BF16)
    cc = jnp.concatenate([bdiag_out(c_re.astype(F32)), -bdiag_out(c_im.astype(F32))], axis=1).astype(BF16)
    a = jnp.stack([ar.reshape(2, g * n), ai.reshape(2, g * n)], axis=1)
    return bb, cc, a


def _win_kernel(sink_ref, q_ref, kl_ref, kc_ref, kr_ref, kx_ref, vl_ref, vc_ref, vr_ref, vx_ref, o_ref,
                *, nctx_blk, nblk):
    jb = pl.program_id(1)
    blk = ATT_BLK
    n_ctx = kx_ref.shape[0]
    grp = WIN_HEADS // WIN_KV_HEADS
    is_lat = jb >= nctx_blk
    ok_l = jnp.logical_and(is_lat, jb - 1 >= nctx_blk)
    ok_r = jnp.logical_and(is_lat, jb + 1 <= nblk - 1)
    nk = 3 * blk + n_ctx
    qi = lax.broadcasted_iota(jnp.int32, (grp * blk, nk), 0) & (blk - 1)
    kj = lax.broadcasted_iota(jnp.int32, (grp * blk, nk), 1)
    band = jnp.abs(kj - blk - qi) <= WINDOW
    i32 = lambda v: v.astype(jnp.int32)
    ok = jnp.where(kj < blk, i32(ok_l), jnp.where(kj < 2 * blk, i32(is_lat), i32(ok_r))) > 0
    mask = jnp.logical_or(kj >= 3 * blk, jnp.logical_and(band, ok))
    ri = lax.broadcasted_iota(jnp.int32, (grp * blk, 1), 0)
    low = lax.broadcasted_iota(jnp.int32, (blk, LANES), 1) < HEAD_DIM
    zero = jnp.zeros((blk, LANES), BF16)
    for h in range(WIN_KV_HEADS):
        hs = slice(h * LANES, (h + 1) * LANES)
        parts = []
        for c in range(grp // 2):
            qc = q_ref[:, (h * (grp // 2) + c) * LANES:(h * (grp // 2) + c + 1) * LANES]
            parts += [jnp.where(low, qc, zero), jnp.where(low, zero, qc)]
        lhs = jnp.concatenate(parts, axis=0)
        keys = jnp.concatenate([kl_ref[:, hs], kc_ref[:, hs], kr_ref[:, hs], kx_ref[:, hs]], axis=0)
        vals = jnp.concatenate([vl_ref[:, hs], vc_ref[:, hs], vr_ref[:, hs], vx_ref[:, hs]], axis=0)
        s = lax.dot_general(lhs, keys, (((1,), (1,)), ((), ())), preferred_element_type=F32)
        s = jnp.where(mask, s, NEG_INF)
        sk = jnp.full((grp * blk, 1), sink_ref[h * grp + grp - 1], F32)
        for g in range(grp - 2, -1, -1):
            sk = jnp.where(ri < (g + 1) * blk, sink_ref[h * grp + g], sk)
        m = jnp.maximum(jnp.max(s, axis=-1, keepdims=True), sk)
        p = jnp.exp(s - m)
        den = jnp.sum(p, axis=-1, keepdims=True) + jnp.exp(sk - m)
        o = jnp.dot(p.astype(BF16), vals, preferred_element_type=F32) * (1.0 / den)
        for c in range(grp // 2):
            oc = jnp.where(low, o[(2 * c) * blk:(2 * c + 1) * blk], o[(2 * c + 1) * blk:(2 * c + 2) * blk])
            o_ref[:, (h * (grp // 2) + c) * LANES:(h * (grp // 2) + c + 1) * LANES] = oc.astype(BF16)


def _window_attention(wq, wk, wv, sink, n_ctx):
    bsz, t, w = wq.shape
    blk = ATT_BLK
    nblk, nctx_blk = t // blk, n_ctx // blk
    kw = wk.shape[-1]
    kspec = lambda f: pl.BlockSpec((None, blk, kw), lambda b, j: (b, f(j), 0))
    left = lambda j: jnp.maximum(j - 1, 0)
    right = lambda j: jnp.minimum(j + 1, nblk - 1)
    ctx = pl.BlockSpec((None, n_ctx, kw), lambda b, j: (b, 0, 0))
    return pl.pallas_call(
        functools.partial(_win_kernel, nctx_blk=nctx_blk, nblk=nblk),
        grid=(bsz, nblk),
        in_specs=[pl.BlockSpec(memory_space=pltpu.SMEM),
                  pl.BlockSpec((None, blk, w), lambda b, j: (b, j, 0)),
                  kspec(left), kspec(lambda j: j), kspec(right), ctx,
                  kspec(left), kspec(lambda j: j), kspec(right), ctx],
        out_specs=pl.BlockSpec((None, blk, w), lambda b, j: (b, j, 0)),
        out_shape=jax.ShapeDtypeStruct((bsz, t, w), BF16),
        compiler_params=_params(("parallel", "parallel")),
        name="window_attention",
    )(sink, wq, wk, wk, wk, wk, wv, wv, wv, wv)


NA_KBLK = 5


def _na_kernel(tab_ref, q_ref, k0, k1, k2, k3, k4, kx_ref, v0, v1, v2, v3, v4, vx_ref, o_ref, bias_scr,
               *, nctx_blk, nlat_blk, rows):
    jb = pl.program_id(0)
    blk = ATT_BLK
    rpb = blk // GRID_W
    jl = jb - nctx_blk
    is_lat = jl >= 0
    start = jnp.clip(jl - 2, 0, nlat_blk - NA_KBLK)
    lane = lax.broadcasted_iota(jnp.int32, (GRID_W, LANES), 1)

    @pl.when(pl.program_id(1) == 0)
    def _():
        for rr in range(rpb):
            r = rpb * jl + rr
            rs = jnp.clip(r - NA_KH // 2, 0, rows - NA_KH)
            for kp in range(NA_KBLK):
                kr0 = rpb * (start + kp)
                d0 = kr0 - r + (NA_KH - 1)
                in0 = jnp.logical_and(is_lat, jnp.logical_and(kr0 >= rs, kr0 < rs + NA_KH))
                in1 = jnp.logical_and(is_lat, jnp.logical_and(kr0 + 1 >= rs, kr0 + 1 < rs + NA_KH))
                ti = jnp.clip(d0 + 1, 0, 2 * NA_KH - 1)
                cond = jnp.where(lane < GRID_W, in0.astype(jnp.int32), in1.astype(jnp.int32)) > 0
                for h in range(NA_HEADS):
                    bias_scr[h, rr * GRID_W:(rr + 1) * GRID_W, kp * LANES:(kp + 1) * LANES] = jnp.where(
                        cond, tab_ref[h, ti], NEG_INF)

    low = lax.broadcasted_iota(jnp.int32, (blk, LANES), 1) < HEAD_DIM
    zero = jnp.zeros((blk, LANES), BF16)
    nloc = NA_KBLK * blk
    for pr in range(NA_HEADS // 2):
        cs = slice(pr * LANES, (pr + 1) * LANES)
        qc = q_ref[:, cs]
        lhs = jnp.concatenate([jnp.where(low, qc, zero), jnp.where(low, zero, qc)], axis=0)
        keys = jnp.concatenate([k0[:, cs], k1[:, cs], k2[:, cs], k3[:, cs], k4[:, cs], kx_ref[:, cs]], axis=0)
        vals = jnp.concatenate([v0[:, cs], v1[:, cs], v2[:, cs], v3[:, cs], v4[:, cs], vx_ref[:, cs]], axis=0)
        s = lax.dot_general(lhs, keys, (((1,), (1,)), ((), ())), preferred_element_type=F32)
        bias = jnp.concatenate([bias_scr[2 * pr], bias_scr[2 * pr + 1]], axis=0)
        s_loc = s[:, :nloc] + bias
        s_ctx = s[:, nloc:]
        m = jnp.maximum(jnp.max(s_loc, axis=-1, keepdims=True), jnp.max(s_ctx, axis=-1, keepdims=True))
        p_loc = jnp.exp(s_loc - m)
        p_ctx = jnp.exp(s_ctx - m)
        den = jnp.sum(p_loc, axis=-1, keepdims=True) + jnp.sum(p_ctx, axis=-1, keepdims=True)
        p = jnp.concatenate([p_loc, p_ctx], axis=1).astype(BF16)
        o = jnp.dot(p, vals, preferred_element_type=F32) * (1.0 / den)
        o_ref[:, cs] = jnp.where(low, o[:blk], o[blk:]).astype(BF16)


def _na_attention(nq, nk, nv, tab, n_ctx):
    bsz, t, w = nq.shape
    blk = ATT_BLK
    nblk, nctx_blk = t // blk, n_ctx // blk
    nlat_blk = nblk - nctx_blk
    rows = nlat_blk * blk // GRID_W
    assert rows >= NA_KH and nlat_blk >= NA_KBLK and blk == 2 * GRID_W

    def kspec(i):
        def imap(j, b):
            start = jnp.clip(j - nctx_blk - 2, 0, nlat_blk - NA_KBLK)
            return (b, nctx_blk + start + i, 0)
        return pl.BlockSpec((None, blk, w), imap)

    ctx = pl.BlockSpec((None, n_ctx, w), lambda j, b: (b, 0, 0))
    kspecs = [kspec(i) for i in range(NA_KBLK)]
    return pl.pallas_call(
        functools.partial(_na_kernel, nctx_blk=nctx_blk, nlat_blk=nlat_blk, rows=rows),
        grid=(nblk, bsz),
        in_specs=[pl.BlockSpec(tab.shape, lambda j, b: (0, 0, 0, 0)),
                  pl.BlockSpec((None, blk, w), lambda j, b: (b, j, 0))]
                 + kspecs + [ctx] + kspecs + [ctx],
        out_specs=pl.BlockSpec((None, blk, w), lambda j, b: (b, j, 0)),
        out_shape=jax.ShapeDtypeStruct((bsz, t, w), BF16),
        scratch_shapes=[pltpu.VMEM((NA_HEADS, blk, NA_KBLK * blk), F32)],
        compiler_params=_params(("arbitrary", "arbitrary")),
        name="neighbourhood_attention",
    )(tab, nq, *([nk] * NA_KBLK), nk, *([nv] * NA_KBLK), nv)


def _na_bias_table(rpb):
    col = np.arange(GRID_W)
    col_start = np.clip(col - NA_KW // 2, 0, GRID_W - NA_KW)
    col_in = (col[None, :] >= col_start[:, None]) & (col[None, :] < col_start[:, None] + NA_KW)
    d_col = np.clip(col[None, :] - col[:, None], 1 - NA_KW, NA_KW - 1) + (NA_KW - 1)
    t = jnp.where(col_in[None, None], rpb.astype(F32)[:, :, d_col], NEG_INF)
    neg = jnp.full_like(t[:, :1], NEG_INF)
    t = jnp.concatenate([neg, t, neg], axis=1)
    return jnp.concatenate([t[:, :-1], t[:, 1:]], axis=-1)


def _gelu_tanh(x):
    return 0.5 * x * (1.0 + jnp.tanh(math.sqrt(2.0 / math.pi) * (x + 0.044715 * (x * x * x))))


def _merge_kernel(y_ref, u_ref, yw_ref, yn_ref, gt_ref, s_ref, dsk_ref, wglu_ref, wbr_ref, wout_ref, mg_ref,
                  o_ref):
    d = s_ref.shape[-1]
    y = y_ref[0].astype(F32) + y_ref[1].astype(F32) + dsk_ref[...] * u_ref[...].astype(F32)
    g = _gelu_tanh(y)
    ya = g * jax.nn.sigmoid(jnp.dot(g.astype(BF16), wglu_ref[...], preferred_element_type=F32))
    m = gt_ref[:, :d].astype(F32) * jnp.dot(ya.astype(BF16), wbr_ref[0], preferred_element_type=F32)
    m += gt_ref[:, d:2 * d].astype(F32) * jnp.dot(yw_ref[...], wbr_ref[1], preferred_element_type=F32)
    m += gt_ref[:, 2 * d:].astype(F32) * jnp.dot(yn_ref[...], wbr_ref[2], preferred_element_type=F32)
    out = jnp.dot(m.astype(BF16), wout_ref[...], preferred_element_type=F32)
    o_ref[...] = s_ref[...] + mg_ref[...] * out


def _merge(y_ssm, a_tm, y_win, y_na, gates, s, d_skip, w_glu, w_branch, w_out, mod_g, n_ctx):
    bsz, t, d = s.shape
    half = d // 2
    tm = ROW_TILE
    nct = n_ctx // tm
    row = lambda b, i: (b, i, 0)
    c2 = lambda b, i: (0, 0)
    return pl.pallas_call(
        _merge_kernel,
        grid=(bsz, t // tm),
        in_specs=[pl.BlockSpec((2, tm, half), lambda b, i: (0, i, b)),
                  pl.BlockSpec((tm, half), lambda b, i: (i, b)),
                  pl.BlockSpec((None, tm, half), row),
                  pl.BlockSpec((None, tm, half), row),
                  pl.BlockSpec((None, tm, N_BRANCH * d), row),
                  pl.BlockSpec((None, tm, d), row),
                  pl.BlockSpec((1, half), c2),
                  pl.BlockSpec((half, half), c2),
                  pl.BlockSpec((N_BRANCH, half, d), lambda b, i: (0, 0, 0)),
                  pl.BlockSpec((d, d), c2),
                  pl.BlockSpec((None, None, 1, d), lambda b, i: (b, (i >= nct).astype(jnp.int32), 0, 0))],
        out_specs=pl.BlockSpec((None, tm, d), row),
        out_shape=jax.ShapeDtypeStruct((bsz, t, d), F32),
        compiler_params=_params(("parallel", "parallel")),
        name="merge",
    )(y_ssm, a_tm, y_win, y_na, gates, s, d_skip, w_glu, w_branch, w_out, mod_g)


def _split_bf16(x):
    hi = x.astype(BF16)
    return hi, (x - hi.astype(F32)).astype(BF16)


def _router_kernel(s_ref, mod_ref, g_ref, rw_ref, rb_ref, h_ref, idx_ref, gate_ref):
    h = _rms_mod(s_ref[...], g_ref[...], mod_ref)
    h_ref[...] = h
    hh, hl = _split_bf16(h)
    wh, wl = _split_bf16(rw_ref[...])
    logits = (jnp.dot(hh, wh, preferred_element_type=F32) + jnp.dot(hl, wh, preferred_element_type=F32)
              + jnp.dot(hh, wl, preferred_element_type=F32)) + rb_ref[...]
    lane = lax.broadcasted_iota(jnp.int32, logits.shape, 1)
    idx_out = jnp.zeros(logits.shape, jnp.int32)
    val_out = jnp.zeros(logits.shape, F32)
    top = None
    for k in range(TOP_K):
        m = jnp.max(logits, axis=-1, keepdims=True)
        sel = jnp.min(jnp.where(logits == m, lane, LANES), axis=-1, keepdims=True)
        top = m if top is None else top
        idx_out = jnp.where(lane == k, sel, idx_out)
        val_out = jnp.where(lane == k, jnp.exp(m - top), val_out)
        logits = jnp.where(lane == sel, NEG_INF, logits)
    idx_ref[...] = idx_out
    gate_ref[...] = val_out * (1.0 / jnp.sum(val_out, axis=-1, keepdims=True))


def _router(s, mod_f, g, rw, rb, n_ctx):
    bsz, t, d = s.shape
    tm = ROW_TILE
    nt = t // tm
    nct = n_ctx // tm
    c2 = lambda b, i: (0, 0)
    flat = lambda b, i: (b * nt + i, 0)
    return pl.pallas_call(
        _router_kernel,
        grid=(bsz, nt),
        in_specs=[pl.BlockSpec((None, tm, d), lambda b, i: (b, i, 0)),
                  pl.BlockSpec((None, None, 2, d), lambda b, i: (b, (i >= nct).astype(jnp.int32), 0, 0)),
                  pl.BlockSpec((1, d), c2),
                  pl.BlockSpec((d, LANES), c2),
                  pl.BlockSpec((1, LANES), c2)],
        out_specs=[pl.BlockSpec((tm, d), flat), pl.BlockSpec((tm, LANES), flat), pl.BlockSpec((tm, LANES), flat)],
        out_shape=[jax.ShapeDtypeStruct((bsz * t, d), F32),
                   jax.ShapeDtypeStruct((bsz * t, LANES), jnp.int32),
                   jax.ShapeDtypeStruct((bsz * t, LANES), F32)],
        compiler_params=_params(("parallel", "parallel")),
        name="router",
    )(s, mod_f, g, rw, rb)


def _row_copy(src, dst, i, j, sem):
    return pltpu.make_async_copy(src.at[pl.ds(i, 1)], dst.at[pl.ds(j, 1)], sem)


def _gather_kernel(tok_ref, h_ref, xb_ref, sem):
    base = pl.program_id(0) * DMA_ROWS

    def issue(r, c):
        _row_copy(h_ref, xb_ref, tok_ref[base + r], base + r, sem.at[0]).start()
        return c

    def drain(r, c):
        _row_copy(h_ref, xb_ref, 0, 0, sem.at[0]).wait()
        return c

    lax.fori_loop(0, DMA_ROWS, issue, 0)
    lax.fori_loop(0, DMA_ROWS, drain, 0)


def _gather_rows(buf_tok, h):
    n_slots = buf_tok.shape[0]
    d = h.shape[-1]
    return pl.pallas_call(
        _gather_kernel,
        grid_spec=pltpu.PrefetchScalarGridSpec(
            num_scalar_prefetch=1,
            grid=(n_slots // DMA_ROWS,),
            in_specs=[pl.BlockSpec(memory_space=pl.ANY)],
            out_specs=pl.BlockSpec(memory_space=pl.ANY),
            scratch_shapes=[pltpu.SemaphoreType.DMA((1,))]),
        out_shape=jax.ShapeDtypeStruct((n_slots, d), h.dtype),
        compiler_params=pltpu.CompilerParams(dimension_semantics=("arbitrary",), has_side_effects=True),
        name="dispatch_gather",
    )(buf_tok, h)


def _expert_kernel(be_ref, x_ref, wgu_ref, bgu_ref, wd_ref, bd_ref, y_ref, wgu_scr, wd_scr):
    i = pl.program_id(0)
    f = wd_ref.shape[0]
    changed = jnp.logical_or(i == 0, be_ref[i] != be_ref[jnp.maximum(i - 1, 0)])

    @pl.when(changed)
    def _():
        wgu_scr[...] = wgu_ref[...].astype(BF16)
        wd_scr[...] = wd_ref[...].astype(BF16)

    gu = jnp.dot(x_ref[...].astype(BF16), wgu_scr[...], preferred_element_type=F32) + bgu_ref[...]
    g = jnp.minimum(gu[:, :f], SWIGLU_LIMIT)
    u = jnp.clip(gu[:, f:], -SWIGLU_LIMIT, SWIGLU_LIMIT)
    act = (u + 1.0) * (g * jax.nn.sigmoid(SWIGLU_ALPHA * g))
    y_ref[...] = jnp.dot(act.astype(BF16), wd_scr[...], preferred_element_type=F32) + bd_ref[...]


def _experts(block_expert, xb, w_gate_up, b_gate_up, w_down, b_down):
    n_slots, d = xb.shape
    e, _, f2 = w_gate_up.shape
    f = f2 // 2
    return pl.pallas_call(
        _expert_kernel,
        grid_spec=pltpu.PrefetchScalarGridSpec(
            num_scalar_prefetch=1,
            grid=(n_slots // MOE_BLK,),
            in_specs=[pl.BlockSpec((MOE_BLK, d), lambda i, be: (i, 0)),
                      pl.BlockSpec((None, d, f2), lambda i, be: (be[i], 0, 0)),
                      pl.BlockSpec((None, 1, f2), lambda i, be: (be[i], 0, 0)),
                      pl.BlockSpec((None, f, d), lambda i, be: (be[i], 0, 0)),
                      pl.BlockSpec((None, 1, d), lambda i, be: (be[i], 0, 0))],
            out_specs=pl.BlockSpec((MOE_BLK, d), lambda i, be: (i, 0)),
            scratch_shapes=[pltpu.VMEM((d, f2), BF16), pltpu.VMEM((f, d), BF16)]),
        out_shape=jax.ShapeDtypeStruct((n_slots, d), F32),
        compiler_params=_params(("arbitrary",)),
        name="experts",
    )(block_expert, xb, w_gate_up, b_gate_up.reshape(e, 1, f2), w_down, b_down.reshape(e, 1, d))


def _combine_kernel(dest_ref, yb_ref, gate_ref, s_ref, mg_ref, o_ref, buf, sem, *, tm):
    base = pl.program_id(0) * tm

    def issue(r, c):
        for k in range(TOP_K):
            _row_copy(yb_ref, buf.at[k], dest_ref[(base + r) * TOP_K + k], r, sem.at[0]).start()
        return c

    def drain(r, c):
        for k in range(TOP_K):
            _row_copy(yb_ref, buf.at[k], 0, 0, sem.at[0]).wait()
        return c

    lax.fori_loop(0, tm, issue, 0)
    lax.fori_loop(0, tm, drain, 0)
    y = gate_ref[:, 0:1] * buf[0]
    for k in range(1, TOP_K):
        y += gate_ref[:, k:k + 1] * buf[k]
    o_ref[...] = s_ref[...] + mg_ref[...] * y


def _combine(dest, yb, gates, s, mod_g, n_ctx):
    bsz, t, d = s.shape
    tm = ATT_BLK
    nt = t // tm
    nct = n_ctx // tm
    return pl.pallas_call(
        functools.partial(_combine_kernel, tm=tm),
        grid_spec=pltpu.PrefetchScalarGridSpec(
            num_scalar_prefetch=1,
            grid=(bsz * nt,),
            in_specs=[pl.BlockSpec(memory_space=pl.ANY),
                      pl.BlockSpec((tm, LANES), lambda i, ds: (i, 0)),
                      pl.BlockSpec((None, tm, d), lambda i, ds: (i // nt, i % nt, 0)),
                      pl.BlockSpec((None, None, 1, d),
                                   lambda i, ds: (i // nt, (i % nt >= nct).astype(jnp.int32), 0, 0))],
            out_specs=pl.BlockSpec((None, tm, d), lambda i, ds: (i // nt, i % nt, 0)),
            scratch_shapes=[pltpu.VMEM((TOP_K, tm, d), F32), pltpu.SemaphoreType.DMA((1,))]),
        out_shape=jax.ShapeDtypeStruct((bsz, t, d), F32),
        compiler_params=_params(("arbitrary",)),
        name="combine",
    )(dest, yb, gates, s, mod_g)


def _dispatch_plan(idx):
    n = idx.shape[0]
    tk = n * TOP_K
    n_blocks = tk // MOE_BLK + N_EXPERTS
    onehot = (idx[:, :, None] == jnp.arange(N_EXPERTS, dtype=jnp.int32)).astype(jnp.int32)
    per_tok = jnp.sum(onehot, axis=1)
    before = jnp.cumsum(per_tok, axis=0) - per_tok
    counts = before[-1] + per_tok[-1]
    rank = jnp.sum(onehot * before[:, None, :], axis=-1)
    padded = (counts + MOE_BLK - 1) // MOE_BLK * MOE_BLK
    seg_end = jnp.cumsum(padded)
    pad_start = seg_end - padded
    dest = (pad_start[idx] + rank).astype(jnp.int32)
    tok = jnp.broadcast_to(jnp.arange(n, dtype=jnp.int32)[:, None], (n, TOP_K))
    buf_tok = jnp.zeros((n_blocks * MOE_BLK,), jnp.int32).at[dest.reshape(tk)].set(tok.reshape(tk))
    block_expert = jnp.minimum(
        jnp.searchsorted(seg_end, jnp.arange(n_blocks, dtype=jnp.int32) * MOE_BLK, side='right'),
        N_EXPERTS - 1).astype(jnp.int32)
    return dest.reshape(tk), buf_tok, block_expert


def _rope_tables(n_ctx, n_lat):
    quarter = HEAD_DIM // 4
    inv_freq = ROPE_THETA ** (-np.arange(quarter, dtype=np.float64) / quarter)
    pos = np.arange(n_lat)
    lane = np.arange(HEAD_DIM)
    p = np.where(lane[None, :] < HEAD_DIM // 2, (pos // GRID_W)[:, None], (pos % GRID_W)[:, None])
    ang = p * inv_freq[lane % quarter][None, :]
    first = ((lane % (HEAD_DIM // 2)) < quarter)[None, :]
    cos = np.cos(ang)
    sin_a = np.where(first, -np.sin(ang), 0.0)
    sin_b = np.where(first, 0.0, np.sin(ang))

    def full(tbl, ctx_val):
        tbl = np.concatenate([np.full((n_ctx, HEAD_DIM), ctx_val), tbl], axis=0)
        return jnp.asarray(np.tile(tbl, (1, LANES // HEAD_DIM)), F32)

    return full(cos, 1.0), full(sin_a, 0.0), full(sin_b, 0.0)


def _head_avg(width):
    i = np.arange(width) // HEAD_DIM
    return jnp.asarray((i[:, None] == i[None, :]) / HEAD_DIM, BF16)


def _tile_heads(w, width):
    return jnp.tile(w.astype(F32), width // HEAD_DIM).reshape(1, width)


def kernel(x, c, ctx, c_ctx, w_mod, b_mod, norm_mix, norm_ffn, w_in, ssm_lam_re, ssm_lam_im, ssm_log_dt,
           ssm_b_re, ssm_b_im, ssm_c_re, ssm_c_im, ssm_d, ssm_w_glu, win_q_norm, win_k_norm, win_sink,
           na_q_norm, na_k_norm, na_rpb, w_branch, w_out, router_w, router_b, w_gate_up, b_gate_up,
           w_down, b_down):
    bsz, n_lat, d = x.shape
    n_ctx = ctx.shape[1]
    depth = w_mod.shape[0]
    half = d // 2
    t = n_ctx + n_lat

    pad = (-(bsz + 1)) % 8
    cc = jnp.concatenate([c, c_ctx[None], jnp.zeros((pad, d), F32)], axis=0)
    mod = _modulation(cc, w_mod, b_mod).reshape(depth, -1, 6, d)
    mod_l = mod[:, :bsz]
    mod_c = jnp.broadcast_to(mod[:, bsz:bsz + 1], mod_l.shape)
    mod = jnp.stack([mod_c, mod_l], axis=2)

    rope = _rope_tables(n_ctx, n_lat)
    bd = _head_avg(half)
    s = jnp.concatenate([ctx, x], axis=1)

    for li in range(depth):
        w_in_b = w_in[li].astype(BF16)
        qk = (_tile_heads(win_q_norm[li], half), _tile_heads(win_k_norm[li], LANES),
              _tile_heads(na_q_norm[li], half), _tile_heads(na_k_norm[li], half))
        a_tm, wq, wk, wv, nq, nk, nv, gates = _premix(
            s, mod[li, :, :, 0:2], norm_mix[li].reshape(1, d), w_in_b, bd, rope, qk, n_ctx)

        bb, cmat, a = _ssm_params(ssm_lam_re[li], ssm_lam_im[li], ssm_log_dt[li], ssm_b_re[li], ssm_b_im[li],
                                  ssm_c_re[li], ssm_c_im[li])
        y_ssm = _ssm(a_tm.reshape(t * bsz, half), bb, cmat, a, bsz, n_ctx)
        y_win = _window_attention(wq, wk, wv, win_sink[li].astype(F32), n_ctx)
        y_na = _na_attention(nq, nk, nv, _na_bias_table(na_rpb[li]), n_ctx)

        s = _merge(y_ssm.reshape(2, t, bsz * half), a_tm, y_win, y_na, gates, s,
                   ssm_d[li].reshape(1, half).astype(F32), ssm_w_glu[li].astype(BF16),
                   w_branch[li].astype(BF16), w_out[li].astype(BF16), mod[li, :, :, 2:3], n_ctx)

        rw = jnp.pad(router_w[li].astype(F32), ((0, 0), (0, LANES - N_EXPERTS)))
        rb = jnp.pad(router_b[li].astype(F32), (0, LANES - N_EXPERTS), constant_values=NEG_INF).reshape(1, LANES)
        h, idx, gate = _router(s, mod[li, :, :, 3:5], norm_ffn[li].reshape(1, d), rw, rb, n_ctx)
        dest, buf_tok, block_expert = _dispatch_plan(idx[:, :TOP_K])
        xb = _gather_rows(buf_tok, h)
        yb = _experts(block_expert, xb, w_gate_up[li], b_gate_up[li], w_down[li], b_down[li])
        s = _combine(dest, yb, gate, s, mod[li, :, :, 5:6], n_ctx)

    return s[:, n_ctx:]
```
